```python
import jax, jax.numpy as jnp
from jax import lax
import numpy as np

D_MODEL = 2048
BATCH = 4
SEQ = 4096
DEPTH = 2

C_A = D_MODEL // 2
N_FGROUPS = 4
F_GROUP = C_A // N_FGROUPS
C_B = D_MODEL // 2
N_B = 64
H_B = C_B // N_B
R_DECAY = 64
R_ICLR = 64
R_GATE = 160
IN_B = 3 * C_B + 2 * R_DECAY + 2 * R_ICLR + R_GATE
IN_AB = C_A + IN_B
GN_EPS = 6.4e-4
DSA_HEAD_DIM = 128
DSA_HEADS = D_MODEL // DSA_HEAD_DIM
DSA_PATTERNS = ((128, 1), (512, 4), (2048, 16))
NEG_INF = -1e30
N_EXPERTS = 64
TOP_K = 8
N_GROUPS = 8
TOPK_GROUPS = 4
D_EXPERT = D_MODEL // 4
ROUTED_SCALE = 2.5
EXPERT_BLOCK = 128
LN_EPS = 1e-5
ALPHA = (2 * DEPTH) ** 0.25
BETA = (8 * DEPTH) ** -0.25
ADA_SCALE = 0.25

kernel_name = 'hybrid_fourier_rwkv7_dilated_moe_encoder'


def layer_norm(x, g, b):
    xf = x.astype(jnp.float32)
    mu = jnp.mean(xf, -1, keepdims=True)
    var = jnp.mean(jnp.square(xf - mu), -1, keepdims=True)
    return ((xf - mu) * lax.rsqrt(var + LN_EPS) * g + b).astype(x.dtype)


def ada_modulation(c, w, b):
    m = (jax.nn.silu(c) @ w + b)[:, None, :]
    shift, scale, gate = jnp.split(m, 3, axis=-1)
    return shift, scale, gate


def fourier_mix(p):
    Bsz, T, _ = p.shape
    z = p.astype(jnp.float32).reshape(Bsz, T, N_FGROUPS, F_GROUP)
    f = jnp.fft.fft2(z, axes=(1, 3), norm='ortho').real
    return f.reshape(Bsz, T, C_A)


def _wkv_step(S, inp):
    r_t, w_t, k_t, v_t, a_t, b_t = inp
    sa = jnp.einsum('bzhij,bzhj->bzhi', S, a_t)
    S = S * w_t[..., None, :] + sa[..., :, None] * b_t[..., None, :] + v_t[..., :, None] * k_t[..., None, :]
    y = jnp.einsum('bzhij,bzhj->bzhi', S, r_t)
    return S, y


def rwkv7_bidirectional(p, mu_prev, mu_next, w0, w_up, a0, a_up, g_up, k_k, k_a, r_k, lnx_g, lnx_b):
    Bsz, T, _ = p.shape
    f32 = jnp.float32
    p = p.astype(f32)
    prev = jnp.pad(p, ((0, 0), (1, 0), (0, 0)))[:, :-1]
    nxt = jnp.pad(p, ((0, 0), (0, 1), (0, 0)))[:, 1:]
    p = p + mu_prev * (prev - p) + mu_next * (nxt - p)
    cuts = np.cumsum([C_B, C_B, C_B, 2 * R_DECAY, 2 * R_ICLR]).tolist()
    r, k, v, wd, ad, gd = jnp.split(p, cuts, axis=-1)
    wd = wd.reshape(Bsz, T, 2, R_DECAY)
    ad = ad.reshape(Bsz, T, 2, R_ICLR)
    w_raw = w0 + jnp.einsum('btzr,zrc->btzc', jnp.tanh(wd), w_up)
    log_w = -jax.nn.softplus(-w_raw) - 0.5
    decay = jnp.exp(-jnp.exp(log_w))
    iclr = jax.nn.sigmoid(a0 + jnp.einsum('btzr,zrc->btzc', ad, a_up))
    g = jax.nn.sigmoid(gd) @ g_up
    hd = lambda z: z.reshape(z.shape[:-1] + (H_B, N_B))
    kk = hd(k * k_k)
    kk = kk / jnp.maximum(jnp.sqrt(jnp.sum(kk * kk, -1, keepdims=True)), 1e-12)
    r, k, v = hd(r), hd(k), hd(v)
    decay, iclr = hd(decay), hd(iclr)
    k_dir = k[:, :, None] * (1.0 + (iclr - 1.0) * hd(k_a))
    b_dir = kk[:, :, None] * iclr
    a_dir = jnp.broadcast_to(-kk[:, :, None], decay.shape)
    r_dir = jnp.broadcast_to(r[:, :, None], decay.shape)
    v_dir = jnp.broadcast_to(v[:, :, None], decay.shape)
    bonus = jnp.sum(jnp.sum(r[:, :, None] * k_dir * r_k, -1, keepdims=True) * v[:, :, None], axis=2)

    def orient(z):
        return jnp.moveaxis(jnp.stack([z[:, :, 0], z[:, ::-1, 1]], axis=2), 1, 0)

    xs = tuple(orient(z) for z in (r_dir, decay, k_dir, v_dir, a_dir, b_dir))
    S0 = jnp.zeros((Bsz, 2, H_B, N_B, N_B), f32)
    _, ys = lax.scan(_wkv_step, S0, xs)
    ys = jnp.moveaxis(ys, 0, 1)
    y = ys[:, :, 0] + ys[:, ::-1, 1]
    mu = jnp.mean(y, -1, keepdims=True)
    var = jnp.mean(jnp.square(y - mu), -1, keepdims=True)
    y = ((y - mu) * lax.rsqrt(var + GN_EPS)).reshape(Bsz, T, C_B) * lnx_g + lnx_b
    return (y + bonus.reshape(Bsz, T, C_B)) * g


def fourier_rwkv_mixer(u, w_in, mu_prev, mu_next, w0, w_up, a0, a_up, g_up, k_k, k_a, r_k, lnx_g, lnx_b, w_out):
    proj = u @ w_in
    ya = fourier_mix(proj[..., :C_A])
    yb = rwkv7_bidirectional(proj[..., C_A:], mu_prev, mu_next, w0, w_up, a0, a_up, g_up,
                             k_k, k_a, r_k, lnx_g, lnx_b)
    y = jnp.concatenate([ya, yb], axis=-1).astype(u.dtype)
    return y @ w_out


def dilated_branch(q, k, v, window, dil, slopes):
    Bsz, T, H, hdim = q.shape
    side = window // (2 * dil)
    blk = side
    L = T // dil
    nb = -(-L // blk)
    Lp = nb * blk

    def by_residue(z):
        return jnp.swapaxes(z.reshape(Bsz, L, dil, H, hdim), 1, 2)

    qr, kr, vr = by_residue(q), by_residue(k), by_residue(v)
    qb = jnp.pad(qr, ((0, 0), (0, 0), (0, Lp - L), (0, 0), (0, 0))).reshape(Bsz, dil, nb, blk, H, hdim)

    def band(z):
        zp = jnp.pad(z, ((0, 0), (0, 0), (blk, Lp - L + blk), (0, 0), (0, 0))).reshape(Bsz, dil, nb + 2, blk, H, hdim)
        return jnp.concatenate([zp[:, :, :nb], zp[:, :, 1:nb + 1], zp[:, :, 2:]], axis=3)

    kw, vw = band(kr), band(vr)
    q_idx = jnp.arange(nb)[:, None] * blk + jnp.arange(blk)[None, :]
    k_idx = jnp.arange(nb)[:, None] * blk - blk + jnp.arange(3 * blk)[None, :]
    rel = jnp.abs(k_idx[:, None, :] - q_idx[:, :, None])
    valid = (rel <= side) & (k_idx[:, None, :] >= 0) & (k_idx[:, None, :] < L)
    s = jnp.einsum('brnihc,brnjhc->brnhij', qb, kw)
    s = s - slopes[:, None, None] * (dil * rel).astype(jnp.float32)[:, None]
    s = jnp.where(valid[:, None], s, NEG_INF)
    m = jnp.max(s, -1)
    p = jnp.exp(s - m[..., None])
    den = jnp.sum(p, -1)
    o = jnp.einsum('brnhij,brnjhc->brnihc', p, vw)

    def from_residue(z):
        return jnp.swapaxes(z[:, :, :L], 1, 2).reshape((Bsz, T) + z.shape[3:])

    o = from_residue(o.reshape(Bsz, dil, Lp, H, hdim))
    m = from_residue(jnp.swapaxes(m, 3, 4).reshape(Bsz, dil, Lp, H))
    den = from_residue(jnp.swapaxes(den, 3, 4).reshape(Bsz, dil, Lp, H))
    return o, m, den


def dilated_attention_mixer(u, w_qkv, w_out):
    Bsz, T, _ = u.shape
    q, k, v = jnp.split((u @ w_qkv).astype(jnp.float32), 3, axis=-1)
    q, k, v = (z.reshape(Bsz, T, DSA_HEADS, DSA_HEAD_DIM) for z in (q, k, v))
    q = q * DSA_HEAD_DIM ** -0.5
    slopes = 2.0 ** (-8.0 * jnp.arange(1, DSA_HEADS + 1, dtype=jnp.float32) / DSA_HEADS)
    outs, maxes, dens = zip(*[dilated_branch(q, k, v, w, d, slopes) for (w, d) in DSA_PATTERNS])
    m = jnp.stack(maxes)
    wts = jnp.exp(m - jnp.max(m, 0))
    num = jnp.einsum('gbth,gbthc->bthc', wts, jnp.stack(outs))
    den = jnp.sum(wts * jnp.stack(dens), 0)
    o = num / den[..., None]
    return o.reshape(Bsz, T, D_MODEL).astype(u.dtype) @ w_out


def swiglu(x, wg, wu, wd):
    return (jax.nn.silu(x @ wg) * (x @ wu)) @ wd


def route(xf, w_router, b_router):
    N = xf.shape[0]
    scores = jax.nn.sigmoid((xf @ w_router).astype(jnp.float32))
    biased = scores + b_router.astype(jnp.float32)
    grp = biased.reshape(N, N_GROUPS, N_EXPERTS // N_GROUPS)
    grp_score = jnp.sum(lax.top_k(grp, 2)[0], -1)
    _, gidx = lax.top_k(grp_score, TOPK_GROUPS)
    gmask = jnp.any(gidx[:, :, None] == jnp.arange(N_GROUPS)[None, None, :], axis=1)
    emask = jnp.repeat(gmask, N_EXPERTS // N_GROUPS, axis=1)
    _, idx = lax.top_k(jnp.where(emask, biased, -jnp.inf), TOP_K)
    w = jnp.take_along_axis(scores, idx, axis=1)
    w = w / jnp.sum(w, -1, keepdims=True) * ROUTED_SCALE
    return idx, w.astype(xf.dtype)


def routed_experts(xf, idx, gate_w, w_gate, w_up, w_down):
    N, D = xf.shape
    M = N * TOP_K
    e_flat = idx.reshape(M)
    tok_flat = jnp.arange(M, dtype=jnp.int32) // TOP_K
    g_flat = gate_w.reshape(M)
    order = jnp.argsort(e_flat)
    e_sorted, tok_sorted, g_sorted = e_flat[order], tok_flat[order], g_flat[order]
    counts = jnp.bincount(e_flat, length=N_EXPERTS)
    starts = jnp.cumsum(counts) - counts
    padded = (counts + EXPERT_BLOCK - 1) // EXPERT_BLOCK * EXPERT_BLOCK
    pends = jnp.cumsum(padded)
    pstarts = pends - padded
    dest = pstarts[e_sorted] + (jnp.arange(M) - starts[e_sorted])
    NB = -(-M // EXPERT_BLOCK) + N_EXPERTS
    P = NB * EXPERT_BLOCK
    buf_tok = jnp.full((P,), N, jnp.int32).at[dest].set(tok_sorted)
    buf_g = jnp.zeros((P,), xf.dtype).at[dest].set(g_sorted)
    block_e = jnp.minimum(jnp.searchsorted(pends, jnp.arange(NB) * EXPERT_BLOCK, side='right'),
                          N_EXPERTS - 1).astype(jnp.int32)
    x_pad = jnp.concatenate([xf, jnp.zeros((1, D), xf.dtype)], axis=0)

    def body(acc, blk):
        tok, g, e = blk
        xb = x_pad[tok]
        h = jax.nn.silu(xb @ w_gate[e]) * (xb @ w_up[e])
        y = (h @ w_down[e]) * g[:, None]
        return acc.at[tok].add(y), None

    acc, _ = lax.scan(body, jnp.zeros((N + 1, D), xf.dtype),
                      (buf_tok.reshape(NB, EXPERT_BLOCK), buf_g.reshape(NB, EXPERT_BLOCK), block_e))
    return acc[:N]


def moe_ffn(u, w_router, b_router, w_gate, w_up, w_down, s_gate, s_up, s_down):
    Bsz, T, D = u.shape
    xf = u.reshape(Bsz * T, D)
    idx, w = route(xf, w_router, b_router)
    y = routed_experts(xf, idx, w, w_gate, w_up, w_down) + swiglu(xf, s_gate, s_up, s_down)
    return y.reshape(Bsz, T, D)


def setup_inputs(seed: int = 0) -> dict:
    key = jax.random.key(seed)
    ks = iter(jax.random.split(key, 48))
    nrm = lambda shape, scale: jax.random.normal(next(ks), shape, jnp.float32) * scale
    uni = lambda shape: jax.random.uniform(next(ks), shape, jnp.float32, 0.0, 0.5)
    NE = (DEPTH + 1) // 2
    NO = DEPTH // 2
    L = DEPTH
    D = D_MODEL
    return {
        'x': nrm((BATCH, SEQ, D), 1.0),
        'c': nrm((BATCH, D), 1.0),
        'ab_w_in': nrm((NE, D, IN_AB), D ** -0.5),
        'ab_mu_prev': uni((NE, IN_B)),
        'ab_mu_next': uni((NE, IN_B)),
        'ab_w0': -2.0 + nrm((NE, 2, C_B), 1.0),
        'ab_w_up': nrm((NE, 2, R_DECAY, C_B), 0.5 * R_DECAY ** -0.5),
        'ab_a0': nrm((NE, 2, C_B), 0.5),
        'ab_a_up': nrm((NE, 2, R_ICLR, C_B), 0.5 * R_ICLR ** -0.5),
        'ab_g_up': nrm((NE, R_GATE, C_B), R_GATE ** -0.5),
        'ab_k_k': 1.0 + nrm((NE, C_B), 0.1),
        'ab_k_a': 1.0 + nrm((NE, C_B), 0.1),
        'ab_r_k': nrm((NE, H_B, N_B), 0.1),
        'ab_lnx_g': 1.0 + nrm((NE, C_B), 0.05),
        'ab_lnx_b': nrm((NE, C_B), 0.02),
        'ab_w_out': nrm((NE, C_A + C_B, D), (C_A + C_B) ** -0.5 * BETA),
        'dsa_w_qkv': nrm((NO, D, 3 * D), D ** -0.5),
        'dsa_w_out': nrm((NO, D, D), D ** -0.5 * BETA),
        'ada_mix_w': nrm((L, D, 3 * D), ADA_SCALE * D ** -0.5),
        'ada_mix_b': nrm((L, 3 * D), 0.02),
        'ln_mix_g': 1.0 + nrm((L, D), 0.05),
        'ln_mix_b': nrm((L, D), 0.02),
        'ada_ffn_w': nrm((L, D, 3 * D), ADA_SCALE * D ** -0.5),
        'ada_ffn_b': nrm((L, 3 * D), 0.02),
        'ln_ffn_g': 1.0 + nrm((L, D), 0.05),
        'ln_ffn_b': nrm((L, D), 0.02),
        'router_w': nrm((L, D, N_EXPERTS), D ** -0.5),
        'router_b': nrm((L, N_EXPERTS), 0.01),
        'exp_gate': nrm((L, N_EXPERTS, D, D_EXPERT), D ** -0.5),
        'exp_up': nrm((L, N_EXPERTS, D, D_EXPERT), D ** -0.5),
        'exp_down': nrm((L, N_EXPERTS, D_EXPERT, D), D_EXPERT ** -0.5 * BETA),
        'sh_gate': nrm((L, D, D_EXPERT), D ** -0.5),
        'sh_up': nrm((L, D, D_EXPERT), D ** -0.5),
        'sh_down': nrm((L, D_EXPERT, D), D_EXPERT ** -0.5 * BETA),
    }


def reference(x, c, ab_w_in, ab_mu_prev, ab_mu_next, ab_w0, ab_w_up, ab_a0, ab_a_up, ab_g_up,
              ab_k_k, ab_k_a, ab_r_k, ab_lnx_g, ab_lnx_b, ab_w_out, dsa_w_qkv, dsa_w_out,
              ada_mix_w, ada_mix_b, ln_mix_g, ln_mix_b, ada_ffn_w, ada_ffn_b, ln_ffn_g, ln_ffn_b,
              router_w, router_b, exp_gate, exp_up, exp_down, sh_gate, sh_up, sh_down):
    for layer in range(DEPTH):
        i = layer // 2
        shift, scale, gate = ada_modulation(c, ada_mix_w[layer], ada_mix_b[layer])
        u = x * (1.0 + scale) + shift
        if layer % 2 == 0:
            h = fourier_rwkv_mixer(u, ab_w_in[i], ab_mu_prev[i], ab_mu_next[i], ab_w0[i], ab_w_up[i],
                                   ab_a0[i], ab_a_up[i], ab_g_up[i], ab_k_k[i], ab_k_a[i], ab_r_k[i],
                                   ab_lnx_g[i], ab_lnx_b[i], ab_w_out[i])
        else:
            h = dilated_attention_mixer(u, dsa_w_qkv[i], dsa_w_out[i])
        x = layer_norm(ALPHA * x + (1.0 + gate) * h, ln_mix_g[layer], ln_mix_b[layer])
        shift, scale, gate = ada_modulation(c, ada_ffn_w[layer], ada_ffn_b[layer])
        u = x * (1.0 + scale) + shift
        h = moe_ffn(u, router_w[layer], router_b[layer], exp_gate[layer], exp_up[layer], exp_down[layer],
                    sh_gate[layer], sh_up[layer], sh_down[layer])
        x = layer_norm(ALPHA * x + (1.0 + gate) * h, ln_ffn_g[layer], ln_ffn_b[layer])
    return x
```

```python
import functools
import math

import jax
import jax.numpy as jnp
import numpy as np
from jax import lax
from jax.experimental import pallas as pl
from jax.experimental.pallas import tpu as pltpu

D_MODEL = 2048
DEPTH = 2
C_A = 1024
N_FGROUPS = 4
F_GROUP = 256
C_B = 1024
N_B = 64
H_B = 16
R_DECAY = 64
R_ICLR = 64
R_GATE = 160
IN_B = 3 * C_B + 2 * R_DECAY + 2 * R_ICLR + R_GATE
IN_B_PAD = 3584
R_GATE_PAD = IN_B_PAD - (3 * C_B + 2 * R_DECAY + 2 * R_ICLR)
GN_EPS = 6.4e-4
DSA_HEAD_DIM = 128
DSA_HEADS = 16
DSA_PATTERNS = ((128, 1), (512, 4), (2048, 16))
DSA_SIDE = 64
NEG_INF = -1e30
N_EXPERTS = 64
TOP_K = 8
N_GROUPS = 8
TOPK_GROUPS = 4
D_EXPERT = 512
ROUTED_SCALE = 2.5
LN_EPS = 1e-5
ALPHA = (2 * DEPTH) ** 0.25

LANES = 128
SUBLANES = 8
VMEM_LIMIT = 56 * 1024 * 1024

F32 = jnp.float32
BF16 = jnp.bfloat16
HIGHEST = lax.Precision.HIGHEST


def _params(*sem, vmem=VMEM_LIMIT):
    return pltpu.CompilerParams(dimension_semantics=tuple(sem), vmem_limit_bytes=vmem)


def _silu(x):
    return x * jax.nn.sigmoid(x)


def _ada_kernel(c_ref, w_ref, b_ref, o_ref):
    c = _silu(c_ref[...])
    o_ref[0] = jnp.dot(c, w_ref[0], precision=HIGHEST, preferred_element_type=F32) + b_ref[0]


def ada_modulation(c, w, b):
    Bsz, D = c.shape
    L, _, N3 = w.shape
    tn = 512
    c_pad = jnp.zeros((SUBLANES, D), F32).at[:Bsz].set(c)
    out = pl.pallas_call(
        _ada_kernel,
        grid=(L, N3 // tn),
        in_specs=[
            pl.BlockSpec((SUBLANES, D), lambda l, j: (0, 0)),
            pl.BlockSpec((1, D, tn), lambda l, j: (l, 0, j)),
            pl.BlockSpec((1, 1, tn), lambda l, j: (l, 0, j)),
        ],
        out_specs=pl.BlockSpec((1, SUBLANES, tn), lambda l, j: (l, 0, j)),
        out_shape=jax.ShapeDtypeStruct((L, SUBLANES, N3), F32),
        compiler_params=_params("parallel", "parallel"),
    )(c_pad, w, b.reshape(L, 1, N3))
    return out[:, :Bsz]


def _split_mod(m):
    shift, scale, gate = jnp.split(m, 3, axis=-1)
    return (1.0 + scale)[:, None, :], shift[:, None, :], (1.0 + gate)[:, None, :]


def _mod_matmul_kernel(x_ref, sc_ref, sh_ref, w_ref, o_ref, u_ref):
    @pl.when(pl.program_id(1) == 0)
    def _():
        u_ref[...] = (x_ref[...] * sc_ref[0] + sh_ref[0]).astype(BF16)

    o_ref[...] = jnp.dot(u_ref[...], w_ref[...], preferred_element_type=F32).astype(o_ref.dtype)


def mod_matmul(x, sc, sh, w, T, out_dtype, tm=1024, tn=512):
    N, D = x.shape
    Nout = w.shape[1]
    tm = min(tm, T)
    per_b = T // tm
    return pl.pallas_call(
        _mod_matmul_kernel,
        grid=(N // tm, Nout // tn),
        in_specs=[
            pl.BlockSpec((tm, D), lambda i, j: (i, 0)),
            pl.BlockSpec((1, 1, D), lambda i, j: (i // per_b, 0, 0)),
            pl.BlockSpec((1, 1, D), lambda i, j: (i // per_b, 0, 0)),
            pl.BlockSpec((D, tn), lambda i, j: (0, j)),
        ],
        out_specs=pl.BlockSpec((tm, tn), lambda i, j: (i, j)),
        out_shape=jax.ShapeDtypeStruct((N, Nout), out_dtype),
        scratch_shapes=[pltpu.VMEM((tm, D), BF16)],
        compiler_params=_params("parallel", "arbitrary"),
    )(x, sc, sh, w)


def _res_ln(x, h, gate1p, g, b):
    y = ALPHA * x + gate1p * h
    mu = jnp.mean(y, axis=-1, keepdims=True)
    yc = y - mu
    var = jnp.mean(yc * yc, axis=-1, keepdims=True)
    return yc * lax.rsqrt(var + LN_EPS) * g + b


def _proj_res_ln_kernel(n_lhs, *refs):
    lhs_refs = refs[:n_lhs]
    w_ref, x_ref, gate_ref, g_ref, b_ref, o_ref = refs[n_lhs:]
    h = None
    k0 = 0
    for lr in lhs_refs:
        kw = lr.shape[1]
        part = jnp.dot(lr[...], w_ref[k0:k0 + kw, :], preferred_element_type=F32)
        h = part if h is None else h + part
        k0 += kw
    o_ref[...] = _res_ln(x_ref[...], h, gate_ref[0], g_ref[...], b_ref[...])


def proj_res_ln(lhs_list, w, x, gate1p, g, b, T, tm=512):
    N, D = x.shape
    tm = min(tm, T)
    per_b = T // tm
    n = len(lhs_list)
    in_specs = [pl.BlockSpec((tm, l.shape[1]), lambda i: (i, 0)) for l in lhs_list]
    in_specs += [
        pl.BlockSpec(w.shape, lambda i: (0, 0)),
        pl.BlockSpec((tm, D), lambda i: (i, 0)),
        pl.BlockSpec((1, 1, D), lambda i: (i // per_b, 0, 0)),
        pl.BlockSpec((1, D), lambda i: (0, 0)),
        pl.BlockSpec((1, D), lambda i: (0, 0)),
    ]
    return pl.pallas_call(
        functools.partial(_proj_res_ln_kernel, n),
        grid=(N // tm,),
        in_specs=in_specs,
        out_specs=pl.BlockSpec((tm, D), lambda i: (i, 0)),
        out_shape=jax.ShapeDtypeStruct((N, D), F32),
        compiler_params=_params("parallel"),
    )(*lhs_list, w, x, gate1p, g.reshape(1, D), b.reshape(1, D))


def _dft_tables(n):
    j = jnp.arange(n, dtype=jnp.int32)
    jk = (j[:, None] * j[None, :]) % n
    ang = jk.astype(F32) * (2.0 * math.pi / n)
    return jnp.cos(ang), jnp.sin(ang)


def _proj_a_kernel(x_ref, sc_ref, sh_ref, w_ref, cs_ref, o_ref):
    u = (x_ref[...] * sc_ref[0] + sh_ref[0]).astype(BF16)
    z = jnp.dot(u, w_ref[...], preferred_element_type=F32).astype(BF16)
    for g in range(N_FGROUPS):
        zg = z[:, g * F_GROUP:(g + 1) * F_GROUP]
        zcs = jnp.dot(zg, cs_ref[...], preferred_element_type=F32)
        o_ref[:, g * F_GROUP:(g + 1) * F_GROUP] = zcs[:, :F_GROUP].astype(BF16)
        o_ref[:, C_A + g * F_GROUP:C_A + (g + 1) * F_GROUP] = zcs[:, F_GROUP:].astype(BF16)


def proj_a_chan_dft(x, sc, sh, w_a, T, tm=512):
    N, D = x.shape
    tm = min(tm, T)
    per_b = T // tm
    cc, ss = _dft_tables(F_GROUP)
    cs = jnp.concatenate([cc, ss], axis=1).astype(BF16)
    return pl.pallas_call(
        _proj_a_kernel,
        grid=(N // tm,),
        in_specs=[
            pl.BlockSpec((tm, D), lambda i: (i, 0)),
            pl.BlockSpec((1, 1, D), lambda i: (i // per_b, 0, 0)),
            pl.BlockSpec((1, 1, D), lambda i: (i // per_b, 0, 0)),
            pl.BlockSpec((D, C_A), lambda i: (0, 0)),
            pl.BlockSpec((F_GROUP, 2 * F_GROUP), lambda i: (0, 0)),
        ],
        out_specs=pl.BlockSpec((tm, 2 * C_A), lambda i: (i, 0)),
        out_shape=jax.ShapeDtypeStruct((N, 2 * C_A), BF16),
        compiler_params=_params("parallel"),
    )(x, sc, sh, w_a, cs)


def _time_dft_kernel(scale, ct_ref, st_ref, zc_ref, zs_ref, o_ref, acc_ref):
    k = pl.program_id(2)

    @pl.when(k == 0)
    def _():
        acc_ref[...] = jnp.zeros_like(acc_ref)

    acc_ref[...] += (jnp.dot(ct_ref[...], zc_ref[...], preferred_element_type=F32)
                     - jnp.dot(st_ref[...], zs_ref[...], preferred_element_type=F32))

    @pl.when(k == pl.num_programs(2) - 1)
    def _():
        o_ref[...] = (acc_ref[...] * scale).astype(o_ref.dtype)


def time_dft(zcs, Bsz, T, tm=1024, tk=512):
    tm = min(tm, T)
    tk = min(tk, T)
    ct, st = _dft_tables(T)
    ct = ct.astype(BF16)
    st = st.astype(BF16)
    nk = T // tk
    scale = 1.0 / math.sqrt(T * F_GROUP)
    return pl.pallas_call(
        functools.partial(_time_dft_kernel, scale),
        grid=(Bsz, T // tm, nk),
        in_specs=[
            pl.BlockSpec((tm, tk), lambda b, i, k: (i, k)),
            pl.BlockSpec((tm, tk), lambda b, i, k: (i, k)),
            pl.BlockSpec((tk, C_A), lambda b, i, k: (b * nk + k, 0)),
            pl.BlockSpec((tk, C_A), lambda b, i, k: (b * nk + k, 1)),
        ],
        out_specs=pl.BlockSpec((tm, C_A), lambda b, i, k: (b * (T // tm) + i, 0)),
        out_shape=jax.ShapeDtypeStruct((Bsz * T, C_A), BF16),
        scratch_shapes=[pltpu.VMEM((tm, C_A), F32)],
        compiler_params=_params("parallel", "parallel", "arbitrary"),
    )(ct, st, zcs, zcs)


def _head_indicator():
    h = jnp.arange(C_B, dtype=jnp.int32) // N_B
    return (h[:, None] == h[None, :]).astype(BF16)


def _segsum(x, e):
    hi = x.astype(BF16)
    lo = (x - hi.astype(F32)).astype(BF16)
    return (jnp.dot(hi, e, preferred_element_type=F32) + jnp.dot(lo, e, preferred_element_type=F32))


def _softplus(z):
    return jnp.maximum(z, 0.0) + jnp.log(1.0 + jnp.exp(-jnp.abs(z)))


def _rwkv_prep_kernel(p_ref, hp_ref, hn_ref, mup_ref, mun_ref, w0_ref, wup_ref, a0_ref, aup_ref, gup_ref,
                      kk_ref, ka_ref, rk_ref, e_ref,
                      r_out, v_out, a_out, w_out, kd_out, b_out, bonus_out, g_out):
    i = pl.program_id(1)
    tt = p_ref.shape[0]
    row = lax.broadcasted_iota(jnp.int32, (tt, 1), 0)
    has_prev = (i > 0).astype(F32)
    has_next = (i < pl.num_programs(1) - 1).astype(F32)

    def shifted(c0, c1):
        p = p_ref[:, c0:c1]
        hp = hp_ref[SUBLANES - 1:SUBLANES, c0:c1] * has_prev
        hn = hn_ref[0:1, c0:c1] * has_next
        prev = jnp.where(row == 0, hp, pltpu.roll(p, 1, axis=0))
        nxt = jnp.where(row == tt - 1, hn, pltpu.roll(p, tt - 1, axis=0))
        return p + mup_ref[:, c0:c1] * (prev - p) + mun_ref[:, c0:c1] * (nxt - p)

    r = shifted(0, C_B)
    k = shifted(C_B, 2 * C_B)
    v = shifted(2 * C_B, 3 * C_B)
    lo = shifted(3 * C_B, IN_B_PAD)
    wd = jnp.tanh(lo[:, :2 * R_DECAY]).astype(BF16)
    ad = lo[:, 2 * R_DECAY:2 * R_DECAY + 2 * R_ICLR].astype(BF16)
    gd = jax.nn.sigmoid(lo[:, 2 * R_DECAY + 2 * R_ICLR:]).astype(BF16)

    e = e_ref[...]
    w_raw = w0_ref[...] + jnp.dot(wd, wup_ref[...], preferred_element_type=F32)
    log_w = -_softplus(-w_raw) - 0.5
    w_out[...] = jnp.exp(-jnp.exp(log_w))
    iclr = jax.nn.sigmoid(a0_ref[...] + jnp.dot(ad, aup_ref[...], preferred_element_type=F32))
    g_out[...] = jnp.dot(gd, gup_ref[...], preferred_element_type=F32)

    kk = k * kk_ref[...]
    kk = kk / jnp.maximum(jnp.sqrt(_segsum(kk * kk, e)), 1e-12)
    bonus = None
    for z in range(2):
        ic = iclr[:, z * C_B:(z + 1) * C_B]
        kd = k * (1.0 + (ic - 1.0) * ka_ref[...])
        kd_out[:, z * C_B:(z + 1) * C_B] = kd
        b_out[:, z * C_B:(z + 1) * C_B] = kk * ic
        bz = _segsum(r * kd * rk_ref[...], e) * v
        bonus = bz if bonus is None else bonus + bz
    r_out[...] = r
    v_out[...] = v
    a_out[...] = -kk
    bonus_out[...] = bonus


def rwkv_prep(pb, Bsz, T, mu_prev, mu_next, w0, w_up, a0, a_up, g_up, k_k, k_a, r_k, tt=256):
    N = Bsz * T
    tt = min(tt, T)
    nt = T // tt
    hb = tt // SUBLANES
    pad = IN_B_PAD - IN_B
    mup = jnp.pad(mu_prev, (0, pad)).reshape(1, IN_B_PAD)
    mun = jnp.pad(mu_next, (0, pad)).reshape(1, IN_B_PAD)
    zeros = jnp.zeros((R_DECAY, C_B), F32)
    wup = jnp.block([[w_up[0], zeros], [zeros, w_up[1]]]).astype(BF16)
    aup = jnp.block([[a_up[0], zeros], [zeros, a_up[1]]]).astype(BF16)
    gup = jnp.pad(g_up, ((0, R_GATE_PAD - R_GATE), (0, 0))).astype(BF16)
    row = lambda a: a.reshape(1, -1)
    n_hblk = N // SUBLANES
    full = lambda shape: pl.BlockSpec(shape, lambda b, i: (0, 0))
    in_specs = [
        pl.BlockSpec((tt, IN_B_PAD), lambda b, i: (b * nt + i, 0)),
        pl.BlockSpec((SUBLANES, IN_B_PAD), lambda b, i: (jnp.maximum((b * nt + i) * hb - 1, 0), 0)),
        pl.BlockSpec((SUBLANES, IN_B_PAD), lambda b, i: (jnp.minimum((b * nt + i + 1) * hb, n_hblk - 1), 0)),
        full((1, IN_B_PAD)), full((1, IN_B_PAD)),
        full((1, 2 * C_B)), full((2 * R_DECAY, 2 * C_B)),
        full((1, 2 * C_B)), full((2 * R_ICLR, 2 * C_B)),
        full((R_GATE_PAD, C_B)),
        full((1, C_B)), full((1, C_B)), full((1, C_B)), full((C_B, C_B)),
    ]
    o1 = pl.BlockSpec((tt, C_B), lambda b, i: (b * nt + i, 0))
    o2 = pl.BlockSpec((tt, 2 * C_B), lambda b, i: (b * nt + i, 0))
    s1 = jax.ShapeDtypeStruct((N, C_B), F32)
    s2 = jax.ShapeDtypeStruct((N, 2 * C_B), F32)
    return pl.pallas_call(
        _rwkv_prep_kernel,
        grid=(Bsz, nt),
        in_specs=in_specs,
        out_specs=[o1, o1, o1, o2, o2, o2, o1, o1],
        out_shape=[s1, s1, s1, s2, s2, s2, s1, s1],
        compiler_params=_params("parallel", "parallel"),
    )(pb, pb, pb, mup, mun, row(w0), wup, row(a0), aup, gup, row(k_k), row(k_a), row(r_k), _head_indicator())


def _wkv_seq_kernel(r_ref, w_ref, k_ref, v_ref, a_ref, b_ref, y_ref, s_ref):
    @pl.when(pl.program_id(0) == 0)
    def _():
        s_ref[...] = jnp.zeros_like(s_ref)

    def step(s, carry):
        r_t = r_ref[s]
        w_t = w_ref[s]
        k_t = k_ref[s]
        a_t = a_ref[s]
        b_t = b_ref[s]

        def row(i, c):
            si = s_ref[i]
            sa = jnp.sum(si * a_t, axis=0, keepdims=True)
            v_i = v_ref[s, pl.ds(i, 1), :]
            sn = si * w_t + sa * b_t + v_i * k_t
            s_ref[i] = sn
            y_ref[s, pl.ds(i, 1), :] = jnp.sum(sn * r_t, axis=0, keepdims=True)
            return c

        return lax.fori_loop(0, N_B, row, carry, unroll=8)

    lax.fori_loop(0, r_ref.shape[0], step, 0)


def wkv_scan(r, w, k, v, a, b, tb=32):
    T, _, nc = r.shape
    tb = min(tb, T)
    spec = pl.BlockSpec((tb, N_B, nc), lambda t: (t, 0, 0))
    return pl.pallas_call(
        _wkv_seq_kernel,
        grid=(T // tb,),
        in_specs=[spec] * 6,
        out_specs=spec,
        out_shape=jax.ShapeDtypeStruct((T, N_B, nc), F32),
        scratch_shapes=[pltpu.VMEM((N_B, N_B, nc), F32)],
        compiler_params=_params("arbitrary"),
    )(r, w, k, v, a, b)


def _to_chains(x, Bsz, T, per_dir):
    if per_dir:
        x = x.reshape(Bsz, T, 2, H_B, N_B)
        x = jnp.stack([x[:, :, 0], x[:, ::-1, 1]], axis=2)
    else:
        x = x.reshape(Bsz, T, H_B, N_B)
        x = jnp.stack([x, x[:, ::-1]], axis=2)
    return jnp.transpose(x, (1, 4, 0, 2, 3)).reshape(T, N_B, Bsz * 2 * H_B)


def _from_chains(y, Bsz, T):
    y = jnp.transpose(y.reshape(T, N_B, Bsz, 2, H_B), (2, 3, 0, 4, 1))
    return y[:, 0].reshape(Bsz * T, C_B), y[:, 1, ::-1].reshape(Bsz * T, C_B)


def _rwkv_post_kernel(yf_ref, yb_ref, bonus_ref, g_ref, lg_ref, lb_ref, e_ref, o_ref):
    e = e_ref[...]
    y = yf_ref[...] + yb_ref[...]
    mu = _segsum(y, e) * (1.0 / N_B)
    yc = y - mu
    var = _segsum(yc * yc, e) * (1.0 / N_B)
    yn = yc * lax.rsqrt(var + GN_EPS) * lg_ref[...] + lb_ref[...]
    o_ref[...] = ((yn + bonus_ref[...]) * g_ref[...]).astype(o_ref.dtype)


def rwkv_post(yf, yb, bonus, g, lnx_g, lnx_b, tm=512):
    N = yf.shape[0]
    tm = min(tm, N)
    blk = pl.BlockSpec((tm, C_B), lambda i: (i, 0))
    vec = pl.BlockSpec((1, C_B), lambda i: (0, 0))
    return pl.pallas_call(
        _rwkv_post_kernel,
        grid=(N // tm,),
        in_specs=[blk, blk, blk, blk, vec, vec, pl.BlockSpec((C_B, C_B), lambda i: (0, 0))],
        out_specs=blk,
        out_shape=jax.ShapeDtypeStruct((N, C_B), BF16),
        compiler_params=_params("parallel"),
    )(yf, yb, bonus, g, lnx_g.reshape(1, C_B), lnx_b.reshape(1, C_B), _head_indicator())


def rwkv7_bidirectional(pb, Bsz, T, mu_prev, mu_next, w0, w_up, a0, a_up, g_up, k_k, k_a, r_k, lnx_g, lnx_b):
    r, v, a, w, kd, b, bonus, g = rwkv_prep(pb, Bsz, T, mu_prev, mu_next, w0.reshape(-1), w_up,
                                            a0.reshape(-1), a_up, g_up, k_k, k_a, r_k.reshape(-1))
    y = wkv_scan(_to_chains(r, Bsz, T, False), _to_chains(w, Bsz, T, True), _to_chains(kd, Bsz, T, True),
                 _to_chains(v, Bsz, T, False), _to_chains(a, Bsz, T, False), _to_chains(b, Bsz, T, True))
    yf, yb = _from_chains(y, Bsz, T)
    return rwkv_post(yf, yb, bonus, g, lnx_g, lnx_b)


def _attn_kernel(dil, L, q_ref, kp_ref, ko_ref, kn_ref, vp_ref, vo_ref, vn_ref, o_ref, lse_ref):
    i = pl.program_id(2)
    tq = q_ref.shape[1]
    S = DSA_SIDE
    nk = tq + 2 * S
    rowi = lax.broadcasted_iota(jnp.int32, (tq, nk), 0)
    coli = lax.broadcasted_iota(jnp.int32, (tq, nk), 1)
    rel = jnp.abs(coli - S - rowi)
    kidx = i * tq - S + coli
    valid = (rel <= S) & (kidx >= 0) & (kidx < L)
    dist = (dil * rel).astype(F32)
    lane = lax.broadcasted_iota(jnp.int32, (tq, LANES), 1)
    lse_all = jnp.zeros((tq, LANES), F32)
    scale = DSA_HEAD_DIM ** -0.5
    for h in range(DSA_HEADS):
        sl = slice(h * DSA_HEAD_DIM, (h + 1) * DSA_HEAD_DIM)
        slope = 2.0 ** (-8.0 * (h + 1) / DSA_HEADS)
        qh = q_ref[0, :, sl]
        kh = jnp.concatenate([kp_ref[0, tq - S:, sl], ko_ref[0, :, sl], kn_ref[0, :S, sl]], axis=0)
        vh = jnp.concatenate([vp_ref[0, tq - S:, sl], vo_ref[0, :, sl], vn_ref[0, :S, sl]], axis=0)
        s = lax.dot_general(qh, kh, (((1,), (1,)), ((), ())), preferred_element_type=F32) * scale
        s = jnp.where(valid, s - slope * dist, NEG_INF)
        m = jnp.max(s, axis=-1, keepdims=True)
        p = jnp.exp(s - m)
        den = jnp.sum(p, axis=-1, keepdims=True)
        o = jnp.dot(p.astype(BF16), vh, preferred_element_type=F32) / den
        o_ref[0, :, sl] = o.astype(o_ref.dtype)
        lse_all = jnp.where(lane == h, m + jnp.log(den), lse_all)
    lse_ref[0] = lse_all


def dilated_branch(qkv, Bsz, T, dil, tq=128):
    D = D_MODEL
    L = T // dil
    tq = min(tq, L)
    nq = L // tq
    qv = qkv.reshape(Bsz, L, dil * 3 * D)
    blk = lambda which, off: pl.BlockSpec(
        (1, tq, D), lambda b, r, i: (b, jnp.clip(i + off, 0, nq - 1), r * 3 + which))
    o, lse = pl.pallas_call(
        functools.partial(_attn_kernel, dil, L),
        grid=(Bsz, dil, nq),
        in_specs=[blk(0, 0), blk(1, -1), blk(1, 0), blk(1, 1), blk(2, -1), blk(2, 0), blk(2, 1)],
        out_specs=[pl.BlockSpec((1, tq, D), lambda b, r, i: (b, i, r)),
                   pl.BlockSpec((1, tq, LANES), lambda b, r, i: (b, i, r))],
        out_shape=[jax.ShapeDtypeStruct((Bsz, L, dil * D), BF16),
                   jax.ShapeDtypeStruct((Bsz, L, dil * LANES), F32)],
        compiler_params=_params("parallel", "parallel", "parallel"),
    )(qv, qv, qv, qv, qv, qv, qv)
    return o.reshape(Bsz * T, D), lse.reshape(Bsz * T, LANES)


def _attn_out_kernel(o1_ref, o2_ref, o3_ref, l1_ref, l2_ref, l3_ref, w_ref, x_ref, gate_ref, g_ref, b_ref,
                     out_ref, om_ref):
    l1, l2, l3 = l1_ref[...], l2_ref[...], l3_ref[...]
    lmax = jnp.maximum(jnp.maximum(l1, l2), l3)
    e1, e2, e3 = jnp.exp(l1 - lmax), jnp.exp(l2 - lmax), jnp.exp(l3 - lmax)
    inv = 1.0 / (e1 + e2 + e3)
    e1, e2, e3 = e1 * inv, e2 * inv, e3 * inv
    for h in range(DSA_HEADS):
        sl = slice(h * DSA_HEAD_DIM, (h + 1) * DSA_HEAD_DIM)
        o = (e1[:, h:h + 1] * o1_ref[:, sl].astype(F32) + e2[:, h:h + 1] * o2_ref[:, sl].astype(F32)
             + e3[:, h:h + 1] * o3_ref[:, sl].astype(F32))
        om_ref[:, sl] = o.astype(BF16)
    hproj = jnp.dot(om_ref[...], w_ref[...], preferred_element_type=F32)
    out_ref[...] = _res_ln(x_ref[...], hproj, gate_ref[0], g_ref[...], b_ref[...])


def attn_merge_proj_res_ln(outs, lses, w, x, gate1p, g, b, T, tm=256):
    N, D = x.shape
    tm = min(tm, T)
    per_b = T // tm
    ob = pl.BlockSpec((tm, D), lambda i: (i, 0))
    lb = pl.BlockSpec((tm, LANES), lambda i: (i, 0))
    vec = pl.BlockSpec((1, D), lambda i: (0, 0))
    return pl.pallas_call(
        _attn_out_kernel,
        grid=(N // tm,),
        in_specs=[ob, ob, ob, lb, lb, lb, pl.BlockSpec((D, D), lambda i: (0, 0)), ob,
                  pl.BlockSpec((1, 1, D), lambda i: (i // per_b, 0, 0)), vec, vec],
        out_specs=ob,
        out_shape=jax.ShapeDtypeStruct((N, D), F32),
        scratch_shapes=[pltpu.VMEM((tm, D), BF16)],
        compiler_params=_params("parallel"),
    )(*outs, *lses, w, x, gate1p, g.reshape(1, D), b.reshape(1, D))


EXPERT_BLK = 256
GROUP_SIZE = N_EXPERTS // N_GROUPS


def _router_kernel(x_ref, sc_ref, sh_ref, wt_ref, rb_ref, idx_ref, wgt_ref):
    tm = x_ref.shape[0]
    u = x_ref[...] * sc_ref[0] + sh_ref[0]
    logits = lax.dot_general(wt_ref[...], u, (((1,), (1,)), ((), ())), precision=HIGHEST,
                             preferred_element_type=F32)
    scores = jax.nn.sigmoid(logits)
    shp = (N_GROUPS, GROUP_SIZE, tm)
    s3 = scores.reshape(shp)
    b3 = (scores + rb_ref[...]).reshape(shp)
    grp = lax.broadcasted_iota(jnp.int32, shp, 0)
    sub = lax.broadcasted_iota(jnp.int32, shp, 1)
    ninf = -jnp.inf
    m1 = jnp.max(b3, axis=1, keepdims=True)
    first = jnp.min(jnp.where(b3 == m1, sub, GROUP_SIZE), axis=1, keepdims=True)
    m2 = jnp.max(jnp.where(sub == first, ninf, b3), axis=1, keepdims=True)
    gs = jnp.broadcast_to(m1 + m2, shp)
    cnt = jnp.zeros(shp, jnp.int32)
    for g2 in range(N_GROUPS):
        other = gs[g2:g2 + 1]
        cnt += ((other > gs) | ((other == gs) & (g2 < grp))).astype(jnp.int32)
    gsel = cnt < TOPK_GROUPS
    masked = jnp.where(gsel, b3, ninf)
    m2d = masked.reshape(N_EXPERTS, tm)
    eidx = grp * GROUP_SIZE + sub
    rank = jnp.zeros(shp, jnp.int32)
    for e2 in range(N_EXPERTS):
        other = m2d[e2:e2 + 1, :][None]
        rank += ((other > masked) | ((other == masked) & (e2 < eidx))).astype(jnp.int32)
    sel = (rank < TOP_K) & gsel

    def total(v):
        return jnp.sum(jnp.sum(v, axis=0), axis=0, keepdims=True)

    ssel = jnp.where(sel, s3, 0.0)
    wd = ssel / total(ssel)[None] * ROUTED_SCALE
    eidx_f = eidx.astype(F32)
    for r in range(TOP_K):
        hit = sel & (rank == r)
        idx_ref[r:r + 1, :] = total(jnp.where(hit, eidx_f, 0.0)).astype(jnp.int32)
        wgt_ref[r:r + 1, :] = total(jnp.where(hit, wd, 0.0))


def route(x, sc, sh, router_w, router_b, T, tm=256):
    N, D = x.shape
    tm = min(tm, T)
    per_b = T // tm
    return pl.pallas_call(
        _router_kernel,
        grid=(N // tm,),
        in_specs=[
            pl.BlockSpec((tm, D), lambda i: (i, 0)),
            pl.BlockSpec((1, 1, D), lambda i: (i // per_b, 0, 0)),
            pl.BlockSpec((1, 1, D), lambda i: (i // per_b, 0, 0)),
            pl.BlockSpec((N_EXPERTS, D), lambda i: (0, 0)),
            pl.BlockSpec((N_EXPERTS, 1), lambda i: (0, 0)),
        ],
        out_specs=[pl.BlockSpec((TOP_K, tm), lambda i: (0, i)), pl.BlockSpec((TOP_K, tm), lambda i: (0, i))],
        out_shape=[jax.ShapeDtypeStruct((TOP_K, N), jnp.int32), jax.ShapeDtypeStruct((TOP_K, N), F32)],
        compiler_params=_params("parallel"),
    )(x, sc, sh, router_w.T, router_b.reshape(N_EXPERTS, 1))


def _dispatch_plan(idx, N):
    M = N * TOP_K
    e_flat = idx.T.reshape(M)
    order = jnp.argsort(e_flat)
    e_sorted = e_flat[order]
    counts = jnp.bincount(e_flat, length=N_EXPERTS)
    starts = jnp.cumsum(counts) - counts
    padded = (counts + EXPERT_BLK - 1) // EXPERT_BLK * EXPERT_BLK
    pends = jnp.cumsum(padded)
    pstarts = pends - padded
    dest = pstarts[e_sorted] + (jnp.arange(M) - starts[e_sorted])
    slot = jnp.zeros((M,), jnp.int32).at[order].set(dest.astype(jnp.int32))
    nb = M // EXPERT_BLK + N_EXPERTS
    block_e = jnp.minimum(jnp.searchsorted(pends, jnp.arange(nb) * EXPERT_BLK, side='right'),
                          N_EXPERTS - 1).astype(jnp.int32)
    n_used = (pends[-1] // EXPERT_BLK).astype(jnp.int32).reshape(1)
    return slot.reshape(N, TOP_K).T, block_e, n_used, nb


def _bf16_pair_pack(lo, hi):
    lo_b = pltpu.bitcast(lo.astype(BF16).astype(F32), jnp.uint32) >> 16
    hi_b = pltpu.bitcast(hi.astype(BF16).astype(F32), jnp.uint32) & jnp.uint32(0xFFFF0000)
    return lo_b | hi_b


def _bf16_pair_unpack(word):
    lo = pltpu.bitcast(word << 16, F32).astype(BF16)
    hi = pltpu.bitcast(word & jnp.uint32(0xFFFF0000), F32).astype(BF16)
    return lo, hi


def _dispatch_kernel(slot_ref, x_ref, sc_ref, sh_ref, xs_in_ref, xs_ref, xp_ref, sem):
    del xs_in_ref
    tm = x_ref.shape[0]
    half = D_MODEL // 2
    u = x_ref[...] * sc_ref[0] + sh_ref[0]
    xp_ref[...] = _bf16_pair_pack(u[:, :half], u[:, half:])

    def row_copy(t, k):
        return pltpu.make_async_copy(xp_ref.at[pl.ds(t, 1)], xs_ref.at[pl.ds(slot_ref[k, t], 1)], sem)

    def start(t, c):
        for k in range(TOP_K):
            row_copy(t, k).start()
        return c

    def wait(t, c):
        for k in range(TOP_K):
            row_copy(t, k).wait()
        return c

    lax.fori_loop(0, tm, start, 0)
    lax.fori_loop(0, tm, wait, 0)


def dispatch(x, sc, sh, slots, P, T, tm=128):
    N, D = x.shape
    tm = min(tm, T)
    per_b = T // tm
    xs0 = jnp.zeros((P, D // 2), jnp.uint32)
    return pl.pallas_call(
        _dispatch_kernel,
        grid=(N // tm,),
        in_specs=[
            pl.BlockSpec((TOP_K, tm), lambda i: (0, i), memory_space=pltpu.SMEM),
            pl.BlockSpec((tm, D), lambda i: (i, 0)),
            pl.BlockSpec((1, 1, D), lambda i: (i // per_b, 0, 0)),
            pl.BlockSpec((1, 1, D), lambda i: (i // per_b, 0, 0)),
            pl.BlockSpec(memory_space=pl.ANY),
        ],
        out_specs=pl.BlockSpec(memory_space=pl.ANY),
        out_shape=jax.ShapeDtypeStruct((P, D // 2), jnp.uint32),
        scratch_shapes=[pltpu.VMEM((tm, D // 2), jnp.uint32), pltpu.SemaphoreType.DMA(())],
        input_output_aliases={4: 0},
        compiler_params=_params("arbitrary"),
    )(slots, x, sc, sh, xs0)


def _expert_kernel(be_ref, nu_ref, xs_ref, wg_ref, wu_ref, wd_ref, y_ref, wg_s, wu_s, wd_s):
    i = pl.program_id(0)
    half = D_MODEL // 2

    @pl.when((i == 0) | (be_ref[i] != be_ref[jnp.maximum(i - 1, 0)]))
    def _():
        wg_s[...] = wg_ref[0, 0].astype(BF16)
        wu_s[...] = wu_ref[0, 0].astype(BF16)
        wd_s[...] = wd_ref[0, 0].astype(BF16)

    @pl.when(i < nu_ref[0])
    def _():
        xa, xb = _bf16_pair_unpack(xs_ref[...])
        gte = (jnp.dot(xa, wg_s[:half], preferred_element_type=F32)
               + jnp.dot(xb, wg_s[half:], preferred_element_type=F32))
        up = (jnp.dot(xa, wu_s[:half], preferred_element_type=F32)
              + jnp.dot(xb, wu_s[half:], preferred_element_type=F32))
        h = (_silu(gte) * up).astype(BF16)
        y_ref[...] = jnp.dot(h, wd_s[...], preferred_element_type=F32)

    @pl.when(i >= nu_ref[0])
    def _():
        y_ref[...] = jnp.zeros_like(y_ref)


def expert_ffn(xs, block_e, n_used, nb, layer, w_gate, w_up, w_down):
    P = xs.shape[0]
    D, F = D_MODEL, D_EXPERT
    grid_spec = pltpu.PrefetchScalarGridSpec(
        num_scalar_prefetch=2,
        grid=(nb,),
        in_specs=[
            pl.BlockSpec((EXPERT_BLK, D // 2), lambda i, be, nu: (i, 0)),
            pl.BlockSpec((1, 1, D, F), lambda i, be, nu: (layer, be[i], 0, 0)),
            pl.BlockSpec((1, 1, D, F), lambda i, be, nu: (layer, be[i], 0, 0)),
            pl.BlockSpec((1, 1, F, D), lambda i, be, nu: (layer, be[i], 0, 0)),
        ],
        out_specs=pl.BlockSpec((EXPERT_BLK, D), lambda i, be, nu: (i, 0)),
        scratch_shapes=[pltpu.VMEM((D, F), BF16), pltpu.VMEM((D, F), BF16), pltpu.VMEM((F, D), BF16)],
    )
    return pl.pallas_call(
        _expert_kernel,
        grid_spec=grid_spec,
        out_shape=jax.ShapeDtypeStruct((P, D), F32),
        compiler_params=_params("arbitrary"),
    )(block_e, n_used, xs, w_gate, w_up, w_down)


def _combine_kernel(slot_ref, wgt_ref, ys_ref, o_ref, buf_ref, sem):
    tm = o_ref.shape[0]

    def row_copy(t, k):
        return pltpu.make_async_copy(ys_ref.at[pl.ds(slot_ref[k, t], 1)], buf_ref.at[k, pl.ds(t, 1)], sem)

    def start(t, c):
        for k in range(TOP_K):
            row_copy(t, k).start()
        return c

    def wait(t, c):
        for k in range(TOP_K):
            row_copy(t, k).wait()
        return c

    lax.fori_loop(0, tm, start, 0)
    lax.fori_loop(0, tm, wait, 0)
    acc = wgt_ref[:, 0:1] * buf_ref[0]
    for k in range(1, TOP_K):
        acc += wgt_ref[:, k:k + 1] * buf_ref[k]
    o_ref[...] = acc


def combine(ys, slots, wgt, N, tm=128):
    D = D_MODEL
    tm = min(tm, N)
    return pl.pallas_call(
        _combine_kernel,
        grid=(N // tm,),
        in_specs=[
            pl.BlockSpec((TOP_K, tm), lambda i: (0, i), memory_space=pltpu.SMEM),
            pl.BlockSpec((tm, TOP_K), lambda i: (i, 0)),
            pl.BlockSpec(memory_space=pl.ANY),
        ],
        out_specs=pl.BlockSpec((tm, D), lambda i: (i, 0)),
        out_shape=jax.ShapeDtypeStruct((N, D), F32),
        scratch_shapes=[pltpu.VMEM((TOP_K, tm, D), F32), pltpu.SemaphoreType.DMA(())],
        compiler_params=_params("arbitrary"),
    )(slots, wgt, ys)


def _shared_ffn_kernel(x_ref, sc_ref, sh_ref, gate_ref, routed_ref, sg_ref, su_ref, sd_ref, g_ref, b_ref, o_ref):
    x = x_ref[...]
    u = (x * sc_ref[0] + sh_ref[0]).astype(BF16)
    hs = (_silu(jnp.dot(u, sg_ref[...], preferred_element_type=F32))
          * jnp.dot(u, su_ref[...], preferred_element_type=F32)).astype(BF16)
    h = jnp.dot(hs, sd_ref[...], preferred_element_type=F32) + routed_ref[...]
    o_ref[...] = _res_ln(x, h, gate_ref[0], g_ref[...], b_ref[...])


def shared_ffn_res_ln(x, sc, sh, gate1p, routed, sg, su, sd, g, b, T, tm=512):
    N, D = x.shape
    F = D_EXPERT
    tm = min(tm, T)
    per_b = T // tm
    xb = pl.BlockSpec((tm, D), lambda i: (i, 0))
    mod = pl.BlockSpec((1, 1, D), lambda i: (i // per_b, 0, 0))
    vec = pl.BlockSpec((1, D), lambda i: (0, 0))
    return pl.pallas_call(
        _shared_ffn_kernel,
        grid=(N // tm,),
        in_specs=[xb, mod, mod, mod, xb,
                  pl.BlockSpec((D, F), lambda i: (0, 0)), pl.BlockSpec((D, F), lambda i: (0, 0)),
                  pl.BlockSpec((F, D), lambda i: (0, 0)), vec, vec],
        out_specs=xb,
        out_shape=jax.ShapeDtypeStruct((N, D), F32),
        compiler_params=_params("parallel"),
    )(x, sc, sh, gate1p, routed, sg, su, sd, g.reshape(1, D), b.reshape(1, D))


def moe_sublayer(x, mod, T, layer, router_w, router_b, w_gate, w_up, w_down, sg, su, sd, ln_g, ln_b):
    N = x.shape[0]
    sc, sh, gate1p = mod
    idx, wgt = route(x, sc, sh, router_w, router_b, T)
    slots, block_e, n_used, nb = _dispatch_plan(idx, N)
    xs = dispatch(x, sc, sh, slots, nb * EXPERT_BLK, T)
    ys = expert_ffn(xs, block_e, n_used, nb, layer, w_gate, w_up, w_down)
    routed = combine(ys, slots, wgt.T, N)
    return shared_ffn_res_ln(x, sc, sh, gate1p, routed, sg.astype(BF16), su.astype(BF16), sd.astype(BF16),
                             ln_g, ln_b, T)


def kernel(x, c, ab_w_in, ab_mu_prev, ab_mu_next, ab_w0, ab_w_up, ab_a0, ab_a_up, ab_g_up, ab_k_k, ab_k_a, ab_r_k, ab_lnx_g, ab_lnx_b, ab_w_out, dsa_w_qkv, dsa_w_out, ada_mix_w, ada_mix_b, ln_mix_g, ln_mix_b, ada_ffn_w, ada_ffn_b, ln_ffn_g, ln_ffn_b, router_w, router_b, exp_gate, exp_up, exp_down, sh_gate, sh_up, sh_down):
    Bsz, T, D = x.shape
    N = Bsz * T
    xf = x.reshape(N, D)
    m_mix = ada_modulation(c, ada_mix_w, ada_mix_b)
    m_ffn = ada_modulation(c, ada_ffn_w, ada_ffn_b)
    for layer in range(DEPTH):
        i = layer // 2
        sc, sh, gate1p = _split_mod(m_mix[layer])
        if layer % 2 == 0:
            w_in = ab_w_in[i]
            w_a = w_in[:, :C_A].astype(BF16)
            w_b = jnp.pad(w_in[:, C_A:], ((0, 0), (0, IN_B_PAD - IN_B))).astype(BF16)
            ya = time_dft(proj_a_chan_dft(xf, sc, sh, w_a, T), Bsz, T)
            pb = mod_matmul(xf, sc, sh, w_b, T, F32)
            yb = rwkv7_bidirectional(pb, Bsz, T, ab_mu_prev[i], ab_mu_next[i], ab_w0[i], ab_w_up[i], ab_a0[i],
                                     ab_a_up[i], ab_g_up[i], ab_k_k[i], ab_k_a[i], ab_r_k[i],
                                     ab_lnx_g[i], ab_lnx_b[i])
            xf = proj_res_ln([ya, yb], ab_w_out[i].astype(BF16), xf, gate1p, ln_mix_g[layer], ln_mix_b[layer], T)
        else:
            qkv = mod_matmul(xf, sc, sh, dsa_w_qkv[i].astype(BF16), T, BF16)
            branches = [dilated_branch(qkv, Bsz, T, dil) for (_, dil) in DSA_PATTERNS]
            xf = attn_merge_proj_res_ln([o for o, _ in branches], [l for _, l in branches],
                                        dsa_w_out[i].astype(BF16), xf, gate1p,
                                        ln_mix_g[layer], ln_mix_b[layer], T)
        xf = moe_sublayer(xf, _split_mod(m_ffn[layer]), T, layer, router_w[layer], router_b[layer],
                          exp_gate, exp_up, exp_down,
                          sh_gate[layer], sh_up[layer], sh_down[layer], ln_ffn_g[layer], ln_ffn_b[layer])
    return xf.reshape(Bsz, T, D)
```
